```python
import math
import jax, jax.numpy as jnp
from jax import lax
import numpy as np

D_MODEL = 1024
BATCH = 2
SEQ = 8192
DEPTH = 4

GRID_W = 64
HEAD_DIM = 64
N_HEADS_NA = 8
N_HEADS_DIL = 8
NA_KH_MAX = 8
NA_KW = 16
DIL_PATTERNS = ((128, 1), (512, 4), (2048, 16))
DIL_BLOCK = 64
N_BUCKETS = 32
T5_MAX_DIST = 1024
CONF_CH = 512
CONF_K = 31
SC_CH = 512
SC_K = 3
D_FF = 2816
FFN_K = 3
LN_EPS = 1e-5
NEG = -1e30
ALPHA = (2 * DEPTH) ** 0.25
BETA = (8 * DEPTH) ** -0.25
N_ATTN_LAYERS = (DEPTH + 1) // 2
N_CONV_LAYERS = DEPTH // 2
ATTN_W = (N_HEADS_NA + N_HEADS_DIL) * HEAD_DIM
CONV_W = CONF_CH + SC_CH

kernel_name = "hybrid_natten_dilated_conformer_shortconv_encoder"


def layer_norm(x, g, b):
    xf = x.astype(jnp.float32)
    mu = jnp.mean(xf, axis=-1, keepdims=True)
    var = jnp.mean(jnp.square(xf - mu), axis=-1, keepdims=True)
    y = (xf - mu) * lax.rsqrt(var + LN_EPS) * g.astype(jnp.float32) + b.astype(jnp.float32)
    return y.astype(x.dtype)


def dwconv(x, w):
    k, c = w.shape
    return lax.conv_general_dilated(
        x, w[:, None, :].astype(x.dtype), window_strides=(1,),
        padding=[(k // 2, k - 1 - k // 2)],
        dimension_numbers=('NWC', 'WIO', 'NWC'), feature_group_count=c)


def t5_bucket(rel):
    nb = N_BUCKETS // 2
    max_exact = nb // 2
    ret = jnp.where(rel > 0, nb, 0)
    n = jnp.abs(rel)
    large = max_exact + (jnp.log(jnp.maximum(n, 1).astype(jnp.float32) / max_exact)
                         / math.log(T5_MAX_DIST / max_exact) * (nb - max_exact)).astype(jnp.int32)
    large = jnp.minimum(large, nb - 1)
    return ret + jnp.where(n < max_exact, n, large)


def neighbourhood_attention(q, k, v, rpb):
    B, S, H, Dh = q.shape
    rows = S // GRID_W
    kh = min(NA_KH_MAX, rows)
    shp = (B, rows, GRID_W, H, Dh)
    qg, kg, vg = q.reshape(shp), k.reshape(shp), v.reshape(shp)
    cols = jnp.arange(GRID_W)
    c0 = jnp.clip(cols - NA_KW // 2, 0, GRID_W - NA_KW)
    col_idx = c0[:, None] + jnp.arange(NA_KW)[None, :]
    col_rel = col_idx - cols[:, None] + NA_KW - 1

    def one_row(r):
        r0 = jnp.clip(r - kh // 2, 0, rows - kh)
        qr = lax.dynamic_index_in_dim(qg, r, axis=1, keepdims=False)
        kr = lax.dynamic_slice_in_dim(kg, r0, kh, axis=1)[:, :, col_idx]
        vr = lax.dynamic_slice_in_dim(vg, r0, kh, axis=1)[:, :, col_idx]
        s = jnp.einsum('bchd,bicjhd->bhcij', qr, kr).astype(jnp.float32)
        row_rel = r0 + jnp.arange(kh) - r + NA_KH_MAX - 1
        bias = rpb[:, row_rel][:, :, col_rel].astype(jnp.float32)
        s = s + jnp.transpose(bias, (0, 2, 1, 3))[None]
        a = jax.nn.softmax(s.reshape(B, H, GRID_W, kh * NA_KW), axis=-1).reshape(s.shape)
        return jnp.einsum('bhcij,bicjhd->bchd', a.astype(vr.dtype), vr)

    out = lax.map(one_row, jnp.arange(rows))
    return jnp.transpose(out, (1, 0, 2, 3, 4)).reshape(B, S, H * Dh)


def dilated_branch(q, k, v, t5_table, window, dil):
    B, S, H, Dh = q.shape
    L = S // dil
    half = window // (2 * dil)
    nblk = -(-L // DIL_BLOCK)
    Lp = nblk * DIL_BLOCK

    def to_res(t):
        t = t.reshape(B, L, dil, H, Dh)
        return jnp.pad(t, ((0, 0), (0, Lp - L), (0, 0), (0, 0), (0, 0)))

    def band(t):
        t = jnp.pad(to_res(t), ((0, 0), (DIL_BLOCK, DIL_BLOCK), (0, 0), (0, 0), (0, 0)))
        t = t.reshape(B, nblk + 2, DIL_BLOCK, dil, H, Dh)
        return jnp.concatenate([t[:, :-2], t[:, 1:-1], t[:, 2:]], axis=2)

    qr = to_res(q).reshape(B, nblk, DIL_BLOCK, dil, H, Dh)
    kb, vb = band(k), band(v)
    s = jnp.einsum('bnqrhd,bnkrhd->bnrhqk', qr, kb).astype(jnp.float32)
    qi = jnp.arange(DIL_BLOCK)
    ki = jnp.arange(3 * DIL_BLOCK)
    rel = ki[None, :] - DIL_BLOCK - qi[:, None]
    bias = jnp.transpose(t5_table[t5_bucket(rel * dil)], (2, 0, 1)).astype(jnp.float32)
    kpos = jnp.arange(nblk)[:, None] * DIL_BLOCK + ki[None, :] - DIL_BLOCK
    valid = ((jnp.abs(rel) <= half)[None]
             & ((kpos >= 0) & (kpos < L))[:, None, :])
    s = jnp.where(valid[None, :, None, None], s + bias, NEG)
    m = jnp.max(s, axis=-1, keepdims=True)
    p = jnp.exp(s - m)
    den = jnp.sum(p, axis=-1, keepdims=True)
    o = jnp.einsum('bnrhqk,bnkrhd->bnqrhd', (p / den).astype(vb.dtype), vb)
    lse = (m + jnp.log(den))[..., 0]
    o = o.reshape(B, Lp, dil, H, Dh)[:, :L].reshape(B, S, H, Dh)
    lse = jnp.transpose(lse, (0, 1, 4, 2, 3)).reshape(B, Lp, dil, H)[:, :L].reshape(B, S, H)
    return o, lse


def dilated_attention(q, k, v, t5_table):
    B, S, H, Dh = q.shape
    res = [dilated_branch(q, k, v, t5_table, w, d) for (w, d) in DIL_PATTERNS]
    outs = jnp.stack([r[0] for r in res]).astype(jnp.float32)
    lses = jnp.stack([r[1] for r in res])
    wts = jax.nn.softmax(lses, axis=0)
    y = jnp.einsum('pbsh,pbshd->bshd', wts, outs)
    return y.astype(q.dtype).reshape(B, S, H * Dh)


def attn_mixer(x, w_in, w_out, rpb, t5_table):
    B, S, _ = x.shape
    h = x @ w_in
    qa, ka, va, qb, kb, vb = jnp.split(h, 6, axis=-1)
    scale = HEAD_DIM ** -0.5
    hs = lambda t, nh: t.reshape(B, S, nh, HEAD_DIM)
    ya = neighbourhood_attention(hs(qa * scale, N_HEADS_NA), hs(ka, N_HEADS_NA), hs(va, N_HEADS_NA), rpb)
    yb = dilated_attention(hs(qb * scale, N_HEADS_DIL), hs(kb, N_HEADS_DIL), hs(vb, N_HEADS_DIL), t5_table)
    return jnp.concatenate([ya, yb], axis=-1) @ w_out


def conv_mixer(x, w_in, conf_dw_w, conf_dw_b, conf_ln_g, conf_ln_b, sconv_w, w_out):
    h = x @ w_in
    ca, cg, gb, gc, hx = jnp.split(
        h, [CONF_CH, 2 * CONF_CH, 2 * CONF_CH + SC_CH, 2 * CONF_CH + 2 * SC_CH], axis=-1)
    u = ca * jax.nn.sigmoid(cg)
    u = dwconv(u, conf_dw_w) + conf_dw_b
    u = jax.nn.silu(layer_norm(u, conf_ln_g, conf_ln_b))
    z = gb * dwconv(gc * hx, sconv_w)
    return jnp.concatenate([u, z], axis=-1) @ w_out


def conv_ffn(x, w_up, dw_w, w_down):
    h = dwconv(x @ w_up, dw_w)
    g, u = jnp.split(h, 2, axis=-1)
    return (jax.nn.silu(g) * u) @ w_down


def setup_inputs(seed: int = 0) -> dict:
    key = jax.random.key(seed)
    ks = jax.random.split(key, 20)
    n = lambda k, s, sc: jax.random.normal(k, s, jnp.float32) * sc
    return {
        "x": n(ks[0], (BATCH, SEQ, D_MODEL), 1.0),
        "t5_bias": n(ks[1], (N_BUCKETS, N_HEADS_DIL), 0.1),
        "attn_w_in": n(ks[2], (N_ATTN_LAYERS, D_MODEL, 3 * ATTN_W), D_MODEL ** -0.5),
        "attn_w_out": n(ks[3], (N_ATTN_LAYERS, ATTN_W, D_MODEL), ATTN_W ** -0.5 * BETA),
        "na_rpb": n(ks[4], (N_ATTN_LAYERS, N_HEADS_NA, 2 * NA_KH_MAX - 1, 2 * NA_KW - 1), 0.1),
        "conv_w_in": n(ks[5], (N_CONV_LAYERS, D_MODEL, 2 * CONF_CH + 3 * SC_CH), D_MODEL ** -0.5),
        "conf_dw_w": n(ks[6], (N_CONV_LAYERS, CONF_K, CONF_CH), CONF_K ** -0.5),
        "conf_dw_b": n(ks[7], (N_CONV_LAYERS, CONF_CH), 0.01),
        "conf_ln_g": 1.0 + n(ks[8], (N_CONV_LAYERS, CONF_CH), 0.01),
        "conf_ln_b": n(ks[9], (N_CONV_LAYERS, CONF_CH), 0.01),
        "sconv_w": n(ks[10], (N_CONV_LAYERS, SC_K, SC_CH), SC_K ** -0.5),
        "conv_w_out": n(ks[11], (N_CONV_LAYERS, CONV_W, D_MODEL), CONV_W ** -0.5 * BETA),
        "ffn_w_up": n(ks[12], (DEPTH, D_MODEL, 2 * D_FF), D_MODEL ** -0.5),
        "ffn_dw_w": n(ks[13], (DEPTH, FFN_K, 2 * D_FF), FFN_K ** -0.5),
        "ffn_w_down": n(ks[14], (DEPTH, D_FF, D_MODEL), D_FF ** -0.5 * BETA),
        "mix_ln_g": 1.0 + n(ks[15], (DEPTH, D_MODEL), 0.01),
        "mix_ln_b": n(ks[16], (DEPTH, D_MODEL), 0.01),
        "ffn_ln_g": 1.0 + n(ks[17], (DEPTH, D_MODEL), 0.01),
        "ffn_ln_b": n(ks[18], (DEPTH, D_MODEL), 0.01),
    }


def reference(x, t5_bias, attn_w_in, attn_w_out, na_rpb, conv_w_in, conf_dw_w, conf_dw_b,
              conf_ln_g, conf_ln_b, sconv_w, conv_w_out, ffn_w_up, ffn_dw_w, ffn_w_down,
              mix_ln_g, mix_ln_b, ffn_ln_g, ffn_ln_b):
    for i in range(DEPTH):
        j = i // 2
        if i % 2 == 0:
            y = attn_mixer(x, attn_w_in[j], attn_w_out[j], na_rpb[j], t5_bias)
        else:
            y = conv_mixer(x, conv_w_in[j], conf_dw_w[j], conf_dw_b[j], conf_ln_g[j],
                           conf_ln_b[j], sconv_w[j], conv_w_out[j])
        x = layer_norm(ALPHA * x + y, mix_ln_g[i], mix_ln_b[i])
        x = layer_norm(ALPHA * x + conv_ffn(x, ffn_w_up[i], ffn_dw_w[i], ffn_w_down[i]),
                       ffn_ln_g[i], ffn_ln_b[i])
    return x
```

```python
import functools
import math

import jax
import jax.numpy as jnp
from jax import lax
from jax.experimental import pallas as pl
from jax.experimental.pallas import tpu as pltpu

GRID_W = 64
HEAD_DIM = 64
N_HEADS = 8
NA_KH = 8
NA_KW = 16
DIL_PATTERNS = ((128, 1), (512, 4), (2048, 16))
DIL_BLOCK = 64
N_BUCKETS = 32
T5_MAX_DIST = 1024
CONF_K = 31
SC_K = 3
FFN_K = 3
LN_EPS = 1e-5
NEG = -1e30
PART = N_HEADS * HEAD_DIM

LANES = 128
SUBLANES_F32 = 8
SUBLANES_BF16 = 16
VMEM_LIMIT = 48 * 1024 * 1024

F32 = jnp.float32
BF16 = jnp.bfloat16


def _cparams(*sem):
    return pltpu.CompilerParams(dimension_semantics=sem, vmem_limit_bytes=VMEM_LIMIT)


def _layer_norm(y, g, b):
    mu = jnp.mean(y, axis=-1, keepdims=True)
    d = y - mu
    var = jnp.mean(d * d, axis=-1, keepdims=True)
    return d * lax.rsqrt(var + LN_EPS) * g + b


def _sigmoid(x):
    return 1.0 / (1.0 + jnp.exp(-x))


def _proj_kernel(x_ref, w_ref, o_ref, xb_ref, *, scaled_parts, scale):
    j = pl.program_id(1)

    @pl.when(j == 0)
    def _():
        xb_ref[...] = x_ref[...].astype(BF16)

    y = jnp.dot(xb_ref[...], w_ref[...], preferred_element_type=F32)
    if scaled_parts:
        is_scaled = functools.reduce(jnp.logical_or, [j == p for p in scaled_parts])
        y = y * jnp.where(is_scaled, scale, 1.0).astype(F32)
    o_ref[...] = y.astype(o_ref.dtype)


def _proj_parts(x, w, *, tm, scaled_parts=(), scale=1.0):
    t, k = x.shape
    n_parts = w.shape[1] // PART
    return pl.pallas_call(
        functools.partial(_proj_kernel, scaled_parts=scaled_parts, scale=scale),
        out_shape=jax.ShapeDtypeStruct((n_parts, t, PART), BF16),
        grid=(t // tm, n_parts),
        in_specs=[pl.BlockSpec((tm, k), lambda i, j: (i, 0)),
                  pl.BlockSpec((k, PART), lambda i, j: (0, j))],
        out_specs=pl.BlockSpec((None, tm, PART), lambda i, j: (j, i, 0)),
        scratch_shapes=[pltpu.VMEM((tm, k), BF16)],
        compiler_params=_cparams("parallel", "arbitrary"),
        name="proj_parts",
    )(x, w)


def _na_kernel(q_ref, k_ref, v_ref, b_ref, o_ref):
    lo = lax.broadcasted_iota(jnp.int32, (GRID_W, LANES), 1) < HEAD_DIM
    n_keys = NA_KH * GRID_W
    for j in range(PART // LANES):
        sl = slice(j * LANES, (j + 1) * LANES)
        qp = q_ref[:, sl]
        kp = k_ref[0, 0, :, :, sl].reshape(n_keys, LANES)
        vp = v_ref[0, 0, :, :, sl].reshape(n_keys, LANES)
        outs = []
        for e in range(2):
            qe = jnp.where(lo if e == 0 else jnp.logical_not(lo), qp, jnp.zeros_like(qp))
            s = lax.dot_general(qe, kp, (((1,), (1,)), ((), ())), preferred_element_type=F32)
            s = s + b_ref[2 * j + e]
            m = jnp.max(s, axis=-1, keepdims=True)
            p = jnp.exp(s - m)
            l = jnp.sum(p, axis=-1, keepdims=True)
            o = jnp.dot(p.astype(BF16), vp, preferred_element_type=F32)
            outs.append(o / l)
        o_ref[:, sl] = jnp.where(lo, outs[0], outs[1]).astype(o_ref.dtype)


def _na_bias_table(rpb):
    var = jnp.arange(NA_KH)
    i = jnp.arange(NA_KH)
    row_rel = i[None, :] + (NA_KH - 1) - var[:, None]
    c = jnp.arange(GRID_W)
    c0 = jnp.clip(c - NA_KW // 2, 0, GRID_W - NA_KW)
    valid = (c[None, :] >= c0[:, None]) & (c[None, :] < c0[:, None] + NA_KW)
    col_rel = jnp.clip(c[None, :] - c[:, None] + NA_KW - 1, 0, 2 * NA_KW - 2)
    tab = rpb.astype(F32)[:, row_rel][:, :, :, col_rel]
    tab = jnp.where(valid[None, None, None], tab, NEG)
    tab = jnp.transpose(tab, (1, 0, 3, 2, 4))
    return tab.reshape(NA_KH, N_HEADS, GRID_W, NA_KH * GRID_W)


def _na_attention(parts, bias_tab, batch, seq):
    rows = seq // GRID_W
    pv = parts.reshape(parts.shape[0], batch, rows, GRID_W, PART)

    def r0(r):
        return jnp.clip(r - NA_KH // 2, 0, rows - NA_KH)

    kv_block = (pl.Element(1), pl.Element(1), pl.Element(NA_KH), pl.Element(GRID_W), pl.Element(PART))
    out = pl.pallas_call(
        _na_kernel,
        out_shape=jax.ShapeDtypeStruct((batch, rows, GRID_W, PART), BF16),
        grid=(batch, rows),
        in_specs=[pl.BlockSpec((None, None, None, GRID_W, PART), lambda b, r: (0, b, r, 0, 0)),
                  pl.BlockSpec(kv_block, lambda b, r: (1, b, r0(r), 0, 0)),
                  pl.BlockSpec(kv_block, lambda b, r: (2, b, r0(r), 0, 0)),
                  pl.BlockSpec((None, N_HEADS, GRID_W, NA_KH * GRID_W),
                               lambda b, r: (r - r0(r), 0, 0, 0))],
        out_specs=pl.BlockSpec((None, None, GRID_W, PART), lambda b, r: (b, r, 0, 0)),
        compiler_params=_cparams("parallel", "arbitrary"),
        name="na_attention",
    )(pv, pv, pv, bias_tab)
    return out.reshape(batch * seq, PART)


def _dil_kernel(q_ref, kp_ref, kc_ref, kn_ref, vp_ref, vc_ref, vn_ref, b_ref, o_ref, lse_ref,
                kx_ref, vx_ref, *, rb, n_row_blocks):
    blk = DIL_BLOCK
    n = pl.program_id(1)
    kx_ref[0:blk, :] = kp_ref[...]
    kx_ref[blk:blk + rb, :] = kc_ref[...]
    kx_ref[blk + rb:, :] = kn_ref[...]
    vx_ref[0:blk, :] = vp_ref[...]
    vx_ref[blk:blk + rb, :] = vc_ref[...]
    vx_ref[blk + rb:, :] = vn_ref[...]

    lo = lax.broadcasted_iota(jnp.int32, (blk, LANES), 1) < HEAD_DIM
    col = lax.broadcasted_iota(jnp.int32, (blk, 3 * blk), 1)
    subs = rb // blk
    last = n_row_blocks * subs - 1

    def body(t, carry):
        gidx = n * subs + t
        pen_lo = jnp.where(gidx == 0, NEG, 0.0).astype(F32)
        pen_hi = jnp.where(gidx == last, NEG, 0.0).astype(F32)
        edge = jnp.where(col < blk, pen_lo, jnp.where(col >= 2 * blk, pen_hi, 0.0))
        row0 = pl.multiple_of(t * blk, blk)
        for j in range(PART // LANES):
            sl = slice(j * LANES, (j + 1) * LANES)
            qp = q_ref[pl.ds(row0, blk), sl]
            kk = kx_ref[pl.ds(row0, 3 * blk), sl]
            vv = vx_ref[pl.ds(row0, 3 * blk), sl]
            outs, lses = [], []
            for e in range(2):
                qe = jnp.where(lo if e == 0 else jnp.logical_not(lo), qp, jnp.zeros_like(qp))
                s = lax.dot_general(qe, kk, (((1,), (1,)), ((), ())), preferred_element_type=F32)
                s = s + b_ref[2 * j + e] + edge
                m = jnp.max(s, axis=-1, keepdims=True)
                p = jnp.exp(s - m)
                l = jnp.sum(p, axis=-1, keepdims=True)
                o = jnp.dot(p.astype(BF16), vv, preferred_element_type=F32)
                outs.append(o / l)
                lses.append(jnp.broadcast_to(m + jnp.log(l), (blk, LANES)))
            o_ref[pl.ds(row0, blk), sl] = jnp.where(lo, outs[0], outs[1]).astype(o_ref.dtype)
            lse_ref[pl.ds(row0, blk), sl] = jnp.where(lo, lses[0], lses[1])
        return carry

    lax.fori_loop(0, subs, body, 0)


def _t5_bucket(rel):
    nb = N_BUCKETS // 2
    max_exact = nb // 2
    ret = jnp.where(rel > 0, nb, 0)
    n = jnp.abs(rel)
    large = max_exact + (jnp.log(jnp.maximum(n, 1).astype(F32) / max_exact)
                         / math.log(T5_MAX_DIST / max_exact) * (nb - max_exact)).astype(jnp.int32)
    large = jnp.minimum(large, nb - 1)
    return ret + jnp.where(n < max_exact, n, large)


def _dil_bias_table(t5_table, window, dil):
    half = window // (2 * dil)
    qi = jnp.arange(DIL_BLOCK)
    ki = jnp.arange(3 * DIL_BLOCK)
    rel = ki[None, :] - DIL_BLOCK - qi[:, None]
    bias = jnp.transpose(t5_table.astype(F32)[_t5_bucket(rel * dil)], (2, 0, 1))
    return jnp.where((jnp.abs(rel) <= half)[None], bias, NEG)


def _dil_attention(parts, bias_tab, batch, seq, dil, *, rb):
    length = seq // dil
    rb = min(rb, length)
    nrb = length // rb
    subs = rb // DIL_BLOCK
    n64 = length // DIL_BLOCK
    pv = parts.reshape(parts.shape[0], batch, length, dil * PART)
    cur = (None, None, rb, PART)
    halo = (None, None, DIL_BLOCK, PART)

    def cur_spec(part):
        return pl.BlockSpec(cur, lambda b, n, r: (part, b, n, r))

    def prev_spec(part):
        return pl.BlockSpec(halo, lambda b, n, r: (part, b, jnp.maximum(n * subs - 1, 0), r))

    def next_spec(part):
        return pl.BlockSpec(halo, lambda b, n, r: (part, b, jnp.minimum((n + 1) * subs, n64 - 1), r))

    out_spec = pl.BlockSpec((None, rb, PART), lambda b, n, r: (b, n, r))
    o, lse = pl.pallas_call(
        functools.partial(_dil_kernel, rb=rb, n_row_blocks=nrb),
        out_shape=(jax.ShapeDtypeStruct((batch, length, dil * PART), BF16),
                   jax.ShapeDtypeStruct((batch, length, dil * PART), F32)),
        grid=(batch, nrb, dil),
        in_specs=[cur_spec(3), prev_spec(4), cur_spec(4), next_spec(4),
                  prev_spec(5), cur_spec(5), next_spec(5),
                  pl.BlockSpec((N_HEADS, DIL_BLOCK, 3 * DIL_BLOCK), lambda b, n, r: (0, 0, 0))],
        out_specs=(out_spec, out_spec),
        scratch_shapes=[pltpu.VMEM((rb + 2 * DIL_BLOCK, PART), BF16),
                        pltpu.VMEM((rb + 2 * DIL_BLOCK, PART), BF16)],
        compiler_params=_cparams("parallel", "parallel", "arbitrary"),
        name=f"dil_attention_d{dil}",
    )(pv, pv, pv, pv, pv, pv, pv, bias_tab)
    return o.reshape(batch * seq, PART), lse.reshape(batch * seq, PART)


def _attn_out_kernel(ya_ref, o1_ref, o2_ref, o3_ref, l1_ref, l2_ref, l3_ref, x_ref, w_ref,
                     g_ref, b_ref, out_ref, *, alpha):
    l1, l2, l3 = l1_ref[...], l2_ref[...], l3_ref[...]
    m = jnp.maximum(jnp.maximum(l1, l2), l3)
    e1, e2, e3 = jnp.exp(l1 - m), jnp.exp(l2 - m), jnp.exp(l3 - m)
    yb = (e1 * o1_ref[...].astype(F32) + e2 * o2_ref[...].astype(F32)
          + e3 * o3_ref[...].astype(F32)) / (e1 + e2 + e3)
    y = jnp.dot(ya_ref[...], w_ref[0:PART, :], preferred_element_type=F32)
    y = y + jnp.dot(yb.astype(BF16), w_ref[PART:, :], preferred_element_type=F32)
    out_ref[...] = _layer_norm(alpha * x_ref[...] + y, g_ref[...], b_ref[...])


def _attn_out(ya, dil_outs, x, w, g, b, *, alpha, tm):
    t, d = x.shape
    half = pl.BlockSpec((tm, PART), lambda i: (i, 0))
    full = pl.BlockSpec((tm, d), lambda i: (i, 0))
    vec = pl.BlockSpec((1, d), lambda i: (0, 0))
    (o1, l1), (o2, l2), (o3, l3) = dil_outs
    return pl.pallas_call(
        functools.partial(_attn_out_kernel, alpha=alpha),
        out_shape=jax.ShapeDtypeStruct((t, d), F32),
        grid=(t // tm,),
        in_specs=[half] * 7 + [full, pl.BlockSpec(w.shape, lambda i: (0, 0)), vec, vec],
        out_specs=full,
        compiler_params=_cparams("parallel"),
        name="attn_out_ln",
    )(ya, o1, o2, o3, l1, l2, l3, x, w, g.reshape(1, d), b.reshape(1, d))


def _conv_out_kernel(a_ref, x_ref, w_ref, g_ref, b_ref, out_ref, *, alpha):
    y = jnp.dot(a_ref[...], w_ref[...], preferred_element_type=F32)
    out_ref[...] = _layer_norm(alpha * x_ref[...] + y, g_ref[...], b_ref[...])


def _conv_out(a, x, w, g, b, *, alpha, tm):
    t, d = x.shape
    full = pl.BlockSpec((tm, d), lambda i: (i, 0))
    vec = pl.BlockSpec((1, d), lambda i: (0, 0))
    return pl.pallas_call(
        functools.partial(_conv_out_kernel, alpha=alpha),
        out_shape=jax.ShapeDtypeStruct((t, d), F32),
        grid=(t // tm,),
        in_specs=[pl.BlockSpec((tm, a.shape[1]), lambda i: (i, 0)), full,
                  pl.BlockSpec(w.shape, lambda i: (0, 0)), vec, vec],
        out_specs=full,
        compiler_params=_cparams("parallel"),
        name="conv_out_ln",
    )(a, x, w, g.reshape(1, d), b.reshape(1, d))


CONV_HALO = 16


def _conv_mix_kernel(ca_p, ca_c, ca_n, cg_p, cg_c, cg_n, gb_c, gc_p, gc_c, gc_n, hx_p, hx_c, hx_n,
                     dw_ref, dwb_ref, lg_ref, lb_ref, sw_ref, o_ref, ub_ref, zb_ref, *, tm, tiles_per_seq):
    h = CONV_HALO
    i = pl.program_id(0)
    keep_prev = jnp.where(i % tiles_per_seq == 0, 0.0, 1.0).astype(F32)
    keep_next = jnp.where(i % tiles_per_seq == tiles_per_seq - 1, 0.0, 1.0).astype(F32)

    def glu(a, g):
        return a[...].astype(F32) * _sigmoid(g[...].astype(F32))

    ub_ref[0:h, :] = glu(ca_p, cg_p) * keep_prev
    ub_ref[h:h + tm, :] = glu(ca_c, cg_c)
    ub_ref[h + tm:, :] = glu(ca_n, cg_n) * keep_next

    def gate(a, g):
        return a[...].astype(F32) * g[...].astype(F32)

    zb_ref[0:h, :] = gate(gc_p, hx_p) * keep_prev
    zb_ref[h:h + tm, :] = gate(gc_c, hx_c)
    zb_ref[h + tm:, :] = gate(gc_n, hx_n) * keep_next

    acc = jnp.zeros((tm, PART), F32)
    for k in range(CONF_K):
        acc = acc + dw_ref[k:k + 1, :] * ub_ref[pl.ds(h - CONF_K // 2 + k, tm), :]
    u = _layer_norm(acc + dwb_ref[...], lg_ref[...], lb_ref[...])
    u = u * _sigmoid(u)

    z = jnp.zeros((tm, PART), F32)
    for k in range(SC_K):
        z = z + sw_ref[k:k + 1, :] * zb_ref[pl.ds(h - SC_K // 2 + k, tm), :]
    z = gb_c[...].astype(F32) * z

    o_ref[:, 0:PART] = u.astype(o_ref.dtype)
    o_ref[:, PART:] = z.astype(o_ref.dtype)


def _conv_mix(parts, dw_w, dw_b, ln_g, ln_b, sconv_w, seq, *, tm):
    t = parts.shape[1]
    h = CONV_HALO
    per = tm // h
    nh = t // h

    def cur(part):
        return pl.BlockSpec((None, tm, PART), lambda i: (part, i, 0))

    def prev(part):
        return pl.BlockSpec((None, h, PART), lambda i: (part, jnp.maximum(i * per - 1, 0), 0))

    def nxt(part):
        return pl.BlockSpec((None, h, PART), lambda i: (part, jnp.minimum((i + 1) * per, nh - 1), 0))

    def whole(a):
        return pl.BlockSpec(a.shape, lambda i: (0,) * a.ndim)

    smalls = [dw_w.astype(F32), dw_b.reshape(1, PART).astype(F32), ln_g.reshape(1, PART).astype(F32),
              ln_b.reshape(1, PART).astype(F32), sconv_w.astype(F32)]
    specs = [prev(0), cur(0), nxt(0), prev(1), cur(1), nxt(1), cur(2),
             prev(3), cur(3), nxt(3), prev(4), cur(4), nxt(4)] + [whole(a) for a in smalls]
    return pl.pallas_call(
        functools.partial(_conv_mix_kernel, tm=tm, tiles_per_seq=seq // tm),
        out_shape=jax.ShapeDtypeStruct((t, 2 * PART), BF16),
        grid=(t // tm,),
        in_specs=specs,
        out_specs=pl.BlockSpec((tm, 2 * PART), lambda i: (i, 0)),
        scratch_shapes=[pltpu.VMEM((tm + 2 * h, PART), F32), pltpu.VMEM((tm + 2 * h, PART), F32)],
        compiler_params=_cparams("parallel"),
        name="conv_mix",
    )(*([parts] * 13), *smalls)


FFN_HALO = 8


def _ffn_kernel(xp_ref, xc_ref, xn_ref, wg_ref, wu_ref, dg_ref, du_ref, wd_ref, g_ref, b_ref,
                out_ref, xe_ref, acc_ref, *, tm, tiles_per_seq, alpha):
    h = FFN_HALO
    i = pl.program_id(0)
    j = pl.program_id(1)

    @pl.when(j == 0)
    def _():
        keep_prev = jnp.where(i % tiles_per_seq == 0, 0.0, 1.0).astype(F32)
        keep_next = jnp.where(i % tiles_per_seq == tiles_per_seq - 1, 0.0, 1.0).astype(F32)
        xe_ref[0:h, :] = (xp_ref[...] * keep_prev).astype(BF16)
        xe_ref[h:h + tm, :] = xc_ref[...].astype(BF16)
        xe_ref[h + tm:, :] = (xn_ref[...] * keep_next).astype(BF16)
        acc_ref[...] = jnp.zeros_like(acc_ref)

    xe = xe_ref[...]
    rows = tm + 2 * h

    def conv(hid, dw_ref):
        below = pltpu.roll(hid, 1, 0)
        above = pltpu.roll(hid, rows - 1, 0)
        c = dw_ref[0:1, :] * below + dw_ref[1:2, :] * hid + dw_ref[2:3, :] * above
        return c[h:h + tm, :]

    cg = conv(jnp.dot(xe, wg_ref[...], preferred_element_type=F32), dg_ref)
    cu = conv(jnp.dot(xe, wu_ref[...], preferred_element_type=F32), du_ref)
    a = (cg * _sigmoid(cg) * cu).astype(BF16)
    acc_ref[...] += jnp.dot(a, wd_ref[...], preferred_element_type=F32)

    @pl.when(j == pl.num_programs(1) - 1)
    def _():
        out_ref[...] = _layer_norm(alpha * xc_ref[...] + acc_ref[...], g_ref[...], b_ref[...])


def _ffn(x, w_up, dw, w_down, g, b, seq, *, alpha, tm, fc):
    t, d = x.shape
    d_ff = w_down.shape[0]
    nch = d_ff // fc
    h = FFN_HALO
    per = tm // h
    nh = t // h
    vec = pl.BlockSpec((1, d), lambda i, j: (0, 0))
    dw = dw.astype(F32)
    return pl.pallas_call(
        functools.partial(_ffn_kernel, tm=tm, tiles_per_seq=seq // tm, alpha=alpha),
        out_shape=jax.ShapeDtypeStruct((t, d), F32),
        grid=(t // tm, nch),
        in_specs=[pl.BlockSpec((h, d), lambda i, j: (jnp.maximum(i * per - 1, 0), 0)),
                  pl.BlockSpec((tm, d), lambda i, j: (i, 0)),
                  pl.BlockSpec((h, d), lambda i, j: (jnp.minimum((i + 1) * per, nh - 1), 0)),
                  pl.BlockSpec((d, fc), lambda i, j: (0, j)),
                  pl.BlockSpec((d, fc), lambda i, j: (0, nch + j)),
                  pl.BlockSpec((FFN_K, fc), lambda i, j: (0, j)),
                  pl.BlockSpec((FFN_K, fc), lambda i, j: (0, nch + j)),
                  pl.BlockSpec((fc, d), lambda i, j: (j, 0)),
                  vec, vec],
        out_specs=pl.BlockSpec((tm, d), lambda i, j: (i, 0)),
        scratch_shapes=[pltpu.VMEM((tm + 2 * h, d), BF16), pltpu.VMEM((tm, d), F32)],
        compiler_params=_cparams("parallel", "arbitrary"),
        name="conv_ffn",
    )(x, x, x, w_up, w_up, dw, dw, w_down, g.reshape(1, d), b.reshape(1, d))


def kernel(x, t5_bias, attn_w_in, attn_w_out, na_rpb, conv_w_in, conf_dw_w, conf_dw_b, conf_ln_g,
           conf_ln_b, sconv_w, conv_w_out, ffn_w_up, ffn_dw_w, ffn_w_down, mix_ln_g, mix_ln_b,
           ffn_ln_g, ffn_ln_b):
    batch, seq, d_model = x.shape
    depth = ffn_w_up.shape[0]
    alpha = (2 * depth) ** 0.25
    q_scale = HEAD_DIM ** -0.5
    xt = x.reshape(batch * seq, d_model)
    dil_tabs = [_dil_bias_table(t5_bias, w, d) for (w, d) in DIL_PATTERNS]

    for i in range(depth):
        j = i // 2
        if i % 2 == 0:
            parts = _proj_parts(xt, attn_w_in[j].astype(BF16), tm=1024,
                                scaled_parts=(0, 3), scale=q_scale)
            ya = _na_attention(parts, _na_bias_table(na_rpb[j]), batch, seq)
            dil_outs = [_dil_attention(parts, tab, batch, seq, d, rb=512)
                        for tab, (_, d) in zip(dil_tabs, DIL_PATTERNS)]
            xt = _attn_out(ya, dil_outs, xt, attn_w_out[j].astype(BF16), mix_ln_g[i], mix_ln_b[i],
                           alpha=alpha, tm=512)
        else:
            parts = _proj_parts(xt, conv_w_in[j].astype(BF16), tm=1024)
            mixed = _conv_mix(parts, conf_dw_w[j], conf_dw_b[j], conf_ln_g[j], conf_ln_b[j],
                              sconv_w[j], seq, tm=512)
            xt = _conv_out(mixed, xt, conv_w_out[j].astype(BF16), mix_ln_g[i], mix_ln_b[i],
                           alpha=alpha, tm=512)
        xt = _ffn(xt, ffn_w_up[i].astype(BF16), ffn_dw_w[i], ffn_w_down[i].astype(BF16),
                  ffn_ln_g[i], ffn_ln_b[i], seq, alpha=alpha, tm=1024, fc=256)
    return xt.reshape(batch, seq, d_model)
```

```python
import functools
import math

import jax
import jax.numpy as jnp
from jax import lax
from jax.experimental import pallas as pl
from jax.experimental.pallas import tpu as pltpu

GRID_W = 64
HEAD_DIM = 64
N_HEADS = 8
NA_KH = 8
NA_KW = 16
DIL_PATTERNS = ((128, 1), (512, 4), (2048, 16))
DIL_BLOCK = 64
N_BUCKETS = 32
T5_MAX_DIST = 1024
CONF_K = 31
SC_K = 3
FFN_K = 3
LN_EPS = 1e-5
NEG = -1e30
PART = N_HEADS * HEAD_DIM

LANES = 128
SUBLANES_F32 = 8
SUBLANES_BF16 = 16
VMEM_LIMIT = 48 * 1024 * 1024

F32 = jnp.float32
BF16 = jnp.bfloat16


def _cparams(*sem):
    return pltpu.CompilerParams(dimension_semantics=sem, vmem_limit_bytes=VMEM_LIMIT)


def _layer_norm(y, g, b):
    mu = jnp.mean(y, axis=-1, keepdims=True)
    d = y - mu
    var = jnp.mean(d * d, axis=-1, keepdims=True)
    return d * lax.rsqrt(var + LN_EPS) * g + b


def _sigmoid(x):
    return 1.0 / (1.0 + jnp.exp(-x))


def _proj_kernel(x_ref, w_ref, o_ref, xb_ref, *, scaled_parts, scale):
    j = pl.program_id(1)

    @pl.when(j == 0)
    def _():
        xb_ref[...] = x_ref[...].astype(BF16)

    y = jnp.dot(xb_ref[...], w_ref[...], preferred_element_type=F32)
    if scaled_parts:
        is_scaled = functools.reduce(jnp.logical_or, [j == p for p in scaled_parts])
        y = y * jnp.where(is_scaled, scale, 1.0).astype(F32)
    o_ref[...] = y.astype(o_ref.dtype)


def _proj_parts(x, w, *, tm, scaled_parts=(), scale=1.0):
    t, k = x.shape
    n_parts = w.shape[1] // PART
    return pl.pallas_call(
        functools.partial(_proj_kernel, scaled_parts=scaled_parts, scale=scale),
        out_shape=jax.ShapeDtypeStruct((n_parts, t, PART), BF16),
        grid=(t // tm, n_parts),
        in_specs=[pl.BlockSpec((tm, k), lambda i, j: (i, 0)),
                  pl.BlockSpec((k, PART), lambda i, j: (0, j))],
        out_specs=pl.BlockSpec((None, tm, PART), lambda i, j: (j, i, 0)),
        scratch_shapes=[pltpu.VMEM((tm, k), BF16)],
        compiler_params=_cparams("parallel", "arbitrary"),
        name="proj_parts",
    )(x, w)


NA_ROWS_PER_STEP = 8
N_PAIRS = PART // LANES


def _stack_pair(qp, lo):
    zero = jnp.zeros_like(qp)
    return jnp.concatenate([jnp.where(lo, qp, zero), jnp.where(lo, zero, qp)], axis=0)


def _pair_scores(qp, kp, bias, lo):
    s = lax.dot_general(_stack_pair(qp, lo), kp, (((1,), (1,)), ((), ())), preferred_element_type=F32)
    return s + bias


def _pair_softmax_pv(s, vp, lo):
    n = s.shape[0] // 2
    m = jnp.max(s, axis=-1, keepdims=True)
    p = jnp.exp(s - m)
    l = jnp.sum(p, axis=-1, keepdims=True)
    o = jnp.dot(p.astype(BF16), vp, preferred_element_type=F32) / l
    return jnp.where(lo, o[:n], o[n:]), m + jnp.log(l)


def _pair_attention(qp, kp, vp, bias, lo):
    return _pair_softmax_pv(_pair_scores(qp, kp, bias, lo), vp, lo)


def _na_kernel(q_ref, k_ref, v_ref, b_ref, o_ref, *, rows):
    rps = q_ref.shape[0]
    band = rps + NA_KH - 1
    g = pl.program_id(1)
    band0 = jnp.clip(g * rps - NA_KH // 2, 0, rows - band)
    lo = lax.broadcasted_iota(jnp.int32, (GRID_W, LANES), 1) < HEAD_DIM
    n_keys = NA_KH * GRID_W
    lanes = [slice(j * LANES, (j + 1) * LANES) for j in range(N_PAIRS)]

    def key_row(rr):
        r = g * rps + rr
        r0 = jnp.clip(r - NA_KH // 2, 0, rows - NA_KH)
        return r0 - band0, r - r0

    def scores(rr):
        off, var = key_row(rr)
        return [_pair_scores(q_ref[rr, :, sl], k_ref[0, 0, pl.ds(off, NA_KH), :, sl].reshape(n_keys, LANES),
                             b_ref[var, j], lo) for j, sl in enumerate(lanes)]

    def finish(rr, s_list):
        off, _ = key_row(rr)
        outs = [_pair_softmax_pv(s, v_ref[0, 0, pl.ds(off, NA_KH), :, sl].reshape(n_keys, LANES), lo)[0]
                for s, sl in zip(s_list, lanes)]
        o_ref[rr] = jnp.concatenate(outs, axis=1).astype(o_ref.dtype)

    s_next = scores(0)
    for rr in range(rps):
        s_cur = s_next
        if rr + 1 < rps:
            s_next = scores(rr + 1)
        finish(rr, s_cur)


def _one_hot_rows(idx, n):
    return (idx[..., None] == jnp.arange(n)).astype(F32)


def _na_bias_table(rpb):
    c = jnp.arange(GRID_W)
    c0 = jnp.clip(c - NA_KW // 2, 0, GRID_W - NA_KW)
    valid = (c[None, :] >= c0[:, None]) & (c[None, :] < c0[:, None] + NA_KW)
    col_rel = c[None, :] - c[:, None] + NA_KW - 1
    sel = _one_hot_rows(col_rel, 2 * NA_KW - 1)
    t = jnp.einsum("hrk,cjk->hrcj", rpb.astype(F32), sel, precision=lax.Precision.HIGHEST)
    t = jnp.where(valid[None, None], t, NEG)
    tab = jnp.stack([t[:, NA_KH - 1 - v:2 * NA_KH - 1 - v] for v in range(NA_KH)])
    tab = jnp.transpose(tab, (0, 1, 3, 2, 4))
    return tab.reshape(NA_KH, N_PAIRS, 2 * GRID_W, NA_KH * GRID_W)


def _na_attention(parts, bias_tab, batch, seq):
    rows = seq // GRID_W
    rps = NA_ROWS_PER_STEP
    band = rps + NA_KH - 1
    pv = parts.reshape(parts.shape[0], batch, rows, GRID_W, PART)

    def band0(g):
        return jnp.clip(g * rps - NA_KH // 2, 0, rows - band)

    kv_block = (pl.Element(1), pl.Element(1), pl.Element(band), pl.Element(GRID_W), pl.Element(PART))
    out = pl.pallas_call(
        functools.partial(_na_kernel, rows=rows),
        out_shape=jax.ShapeDtypeStruct((batch, rows, GRID_W, PART), BF16),
        grid=(batch, rows // rps),
        in_specs=[pl.BlockSpec((None, None, rps, GRID_W, PART), lambda b, g: (0, b, g, 0, 0)),
                  pl.BlockSpec(kv_block, lambda b, g: (1, b, band0(g), 0, 0)),
                  pl.BlockSpec(kv_block, lambda b, g: (2, b, band0(g), 0, 0)),
                  pl.BlockSpec(bias_tab.shape, lambda b, g: (0, 0, 0, 0))],
        out_specs=pl.BlockSpec((None, rps, GRID_W, PART), lambda b, g: (b, g, 0, 0)),
        compiler_params=_cparams("parallel", "arbitrary"),
        name="na_attention",
    )(pv, pv, pv, bias_tab)
    return out.reshape(batch * seq, PART)


def _dil_kernel(q_ref, kp_ref, kc_ref, kn_ref, vp_ref, vc_ref, vn_ref, b_ref, o_ref, lse_ref,
                kx_ref, vx_ref, *, rb, n_row_blocks):
    blk = DIL_BLOCK
    n = pl.program_id(1)
    kx_ref[0:blk, :] = kp_ref[...]
    kx_ref[blk:blk + rb, :] = kc_ref[...]
    kx_ref[blk + rb:, :] = kn_ref[...]
    vx_ref[0:blk, :] = vp_ref[...]
    vx_ref[blk:blk + rb, :] = vc_ref[...]
    vx_ref[blk + rb:, :] = vn_ref[...]

    lo = lax.broadcasted_iota(jnp.int32, (blk, LANES), 1) < HEAD_DIM
    col = lax.broadcasted_iota(jnp.int32, (2 * blk, 3 * blk), 1)
    subs = rb // blk
    last = n_row_blocks * subs - 1

    lanes = [slice(j * LANES, (j + 1) * LANES) for j in range(N_PAIRS)]

    def scores(t):
        gidx = n * subs + t
        pen_lo = jnp.where(gidx == 0, NEG, 0.0).astype(F32)
        pen_hi = jnp.where(gidx == last, NEG, 0.0).astype(F32)
        edge = jnp.where(col < blk, pen_lo, jnp.where(col >= 2 * blk, pen_hi, 0.0))
        return [_pair_scores(q_ref[t * blk:(t + 1) * blk, sl], kx_ref[t * blk:(t + 3) * blk, sl],
                             b_ref[j] + edge, lo) for j, sl in enumerate(lanes)]

    def finish(t, s_list):
        outs, lses = [], []
        for s, sl in zip(s_list, lanes):
            o, lse = _pair_softmax_pv(s, vx_ref[t * blk:(t + 3) * blk, sl], lo)
            outs.append(o)
            lse = jnp.broadcast_to(lse, (2 * blk, LANES))
            lses.append(jnp.where(lo, lse[:blk], lse[blk:]))
        o_ref[t * blk:(t + 1) * blk, :] = jnp.concatenate(outs, axis=1).astype(o_ref.dtype)
        lse_ref[t * blk:(t + 1) * blk, :] = jnp.concatenate(lses, axis=1)

    s_next = scores(0)
    for t in range(subs):
        s_cur = s_next
        if t + 1 < subs:
            s_next = scores(t + 1)
        finish(t, s_cur)


def _t5_bucket(rel):
    nb = N_BUCKETS // 2
    max_exact = nb // 2
    ret = jnp.where(rel > 0, nb, 0)
    n = jnp.abs(rel)
    large = max_exact + (jnp.log(jnp.maximum(n, 1).astype(F32) / max_exact)
                         / math.log(T5_MAX_DIST / max_exact) * (nb - max_exact)).astype(jnp.int32)
    large = jnp.minimum(large, nb - 1)
    return ret + jnp.where(n < max_exact, n, large)


def _dil_bias_table(t5_table, window, dil):
    half = window // (2 * dil)
    qi = jnp.arange(DIL_BLOCK)
    ki = jnp.arange(3 * DIL_BLOCK)
    rel = ki[None, :] - DIL_BLOCK - qi[:, None]
    sel = _one_hot_rows(_t5_bucket(rel * dil), N_BUCKETS)
    bias = jnp.einsum("qkb,bh->hqk", sel, t5_table.astype(F32), precision=lax.Precision.HIGHEST)
    bias = jnp.where((jnp.abs(rel) <= half)[None], bias, NEG)
    return bias.reshape(N_PAIRS, 2 * DIL_BLOCK, 3 * DIL_BLOCK)


def _dil_attention(parts, bias_tab, batch, seq, dil, *, rb):
    length = seq // dil
    rb = min(rb, length)
    nrb = length // rb
    subs = rb // DIL_BLOCK
    n64 = length // DIL_BLOCK
    pv = parts.reshape(parts.shape[0], batch, length, dil * PART)
    cur = (None, None, rb, PART)
    halo = (None, None, DIL_BLOCK, PART)

    def cur_spec(part):
        return pl.BlockSpec(cur, lambda b, n, r: (part, b, n, r))

    def prev_spec(part):
        return pl.BlockSpec(halo, lambda b, n, r: (part, b, jnp.maximum(n * subs - 1, 0), r))

    def next_spec(part):
        return pl.BlockSpec(halo, lambda b, n, r: (part, b, jnp.minimum((n + 1) * subs, n64 - 1), r))

    out_spec = pl.BlockSpec((None, rb, PART), lambda b, n, r: (b, n, r))
    o, lse = pl.pallas_call(
        functools.partial(_dil_kernel, rb=rb, n_row_blocks=nrb),
        out_shape=(jax.ShapeDtypeStruct((batch, length, dil * PART), BF16),
                   jax.ShapeDtypeStruct((batch, length, dil * PART), F32)),
        grid=(batch, nrb, dil),
        in_specs=[cur_spec(3), prev_spec(4), cur_spec(4), next_spec(4),
                  prev_spec(5), cur_spec(5), next_spec(5),
                  pl.BlockSpec(bias_tab.shape, lambda b, n, r: (0, 0, 0))],
        out_specs=(out_spec, out_spec),
        scratch_shapes=[pltpu.VMEM((rb + 2 * DIL_BLOCK, PART), BF16),
                        pltpu.VMEM((rb + 2 * DIL_BLOCK, PART), BF16)],
        compiler_params=_cparams("parallel", "parallel", "arbitrary"),
        name=f"dil_attention_d{dil}",
    )(pv, pv, pv, pv, pv, pv, pv, bias_tab)
    return o.reshape(batch * seq, PART), lse.reshape(batch * seq, PART)


def _attn_out_kernel(ya_ref, o1_ref, o2_ref, o3_ref, l1_ref, l2_ref, l3_ref, x_ref, w_ref,
                     g_ref, b_ref, out_ref, *, alpha):
    l1, l2, l3 = l1_ref[...], l2_ref[...], l3_ref[...]
    m = jnp.maximum(jnp.maximum(l1, l2), l3)
    e1, e2, e3 = jnp.exp(l1 - m), jnp.exp(l2 - m), jnp.exp(l3 - m)
    yb = (e1 * o1_ref[...].astype(F32) + e2 * o2_ref[...].astype(F32)
          + e3 * o3_ref[...].astype(F32)) / (e1 + e2 + e3)
    y = jnp.dot(ya_ref[...], w_ref[0:PART, :], preferred_element_type=F32)
    y = y + jnp.dot(yb.astype(BF16), w_ref[PART:, :], preferred_element_type=F32)
    out_ref[...] = _layer_norm(alpha * x_ref[...] + y, g_ref[...], b_ref[...])


def _attn_out(ya, dil_outs, x, w, g, b, *, alpha, tm):
    t, d = x.shape
    half = pl.BlockSpec((tm, PART), lambda i: (i, 0))
    full = pl.BlockSpec((tm, d), lambda i: (i, 0))
    vec = pl.BlockSpec((1, d), lambda i: (0, 0))
    (o1, l1), (o2, l2), (o3, l3) = dil_outs
    return pl.pallas_call(
        functools.partial(_attn_out_kernel, alpha=alpha),
        out_shape=jax.ShapeDtypeStruct((t, d), F32),
        grid=(t // tm,),
        in_specs=[half] * 7 + [full, pl.BlockSpec(w.shape, lambda i: (0, 0)), vec, vec],
        out_specs=full,
        compiler_params=_cparams("parallel"),
        name="attn_out_ln",
    )(ya, o1, o2, o3, l1, l2, l3, x, w, g.reshape(1, d), b.reshape(1, d))


def _conv_out_kernel(a_ref, x_ref, w_ref, g_ref, b_ref, out_ref, *, alpha):
    y = jnp.dot(a_ref[...], w_ref[...], preferred_element_type=F32)
    out_ref[...] = _layer_norm(alpha * x_ref[...] + y, g_ref[...], b_ref[...])


def _conv_out(a, x, w, g, b, *, alpha, tm):
    t, d = x.shape
    full = pl.BlockSpec((tm, d), lambda i: (i, 0))
    vec = pl.BlockSpec((1, d), lambda i: (0, 0))
    return pl.pallas_call(
        functools.partial(_conv_out_kernel, alpha=alpha),
        out_shape=jax.ShapeDtypeStruct((t, d), F32),
        grid=(t // tm,),
        in_specs=[pl.BlockSpec((tm, a.shape[1]), lambda i: (i, 0)), full,
                  pl.BlockSpec(w.shape, lambda i: (0, 0)), vec, vec],
        out_specs=full,
        compiler_params=_cparams("parallel"),
        name="conv_out_ln",
    )(a, x, w, g.reshape(1, d), b.reshape(1, d))


CONV_HALO = 16


def _conv_mix_kernel(ca_p, ca_c, ca_n, cg_p, cg_c, cg_n, gb_c, gc_p, gc_c, gc_n, hx_p, hx_c, hx_n,
                     dw_ref, dwb_ref, lg_ref, lb_ref, sw_ref, o_ref, ub_ref, zb_ref, *, tm, tiles_per_seq):
    h = CONV_HALO
    i = pl.program_id(0)
    keep_prev = jnp.where(i % tiles_per_seq == 0, 0.0, 1.0).astype(F32)
    keep_next = jnp.where(i % tiles_per_seq == tiles_per_seq - 1, 0.0, 1.0).astype(F32)

    def glu(a, g):
        return a[...].astype(F32) * _sigmoid(g[...].astype(F32))

    ub_ref[0:h, :] = glu(ca_p, cg_p) * keep_prev
    ub_ref[h:h + tm, :] = glu(ca_c, cg_c)
    ub_ref[h + tm:, :] = glu(ca_n, cg_n) * keep_next

    def gate(a, g):
        return a[...].astype(F32) * g[...].astype(F32)

    zb_ref[0:h, :] = gate(gc_p, hx_p) * keep_prev
    zb_ref[h:h + tm, :] = gate(gc_c, hx_c)
    zb_ref[h + tm:, :] = gate(gc_n, hx_n) * keep_next

    acc = jnp.zeros((tm, PART), F32)
    for k in range(CONF_K):
        acc = acc + dw_ref[k:k + 1, :] * ub_ref[pl.ds(h - CONF_K // 2 + k, tm), :]
    u = _layer_norm(acc + dwb_ref[...], lg_ref[...], lb_ref[...])
    u = u * _sigmoid(u)

    z = jnp.zeros((tm, PART), F32)
    for k in range(SC_K):
        z = z + sw_ref[k:k + 1, :] * zb_ref[pl.ds(h - SC_K // 2 + k, tm), :]
    z = gb_c[...].astype(F32) * z

    o_ref[:, 0:PART] = u.astype(o_ref.dtype)
    o_ref[:, PART:] = z.astype(o_ref.dtype)


def _conv_mix(parts, dw_w, dw_b, ln_g, ln_b, sconv_w, seq, *, tm):
    t = parts.shape[1]
    h = CONV_HALO
    per = tm // h
    nh = t // h

    def cur(part):
        return pl.BlockSpec((None, tm, PART), lambda i: (part, i, 0))

    def prev(part):
        return pl.BlockSpec((None, h, PART), lambda i: (part, jnp.maximum(i * per - 1, 0), 0))

    def nxt(part):
        return pl.BlockSpec((None, h, PART), lambda i: (part, jnp.minimum((i + 1) * per, nh - 1), 0))

    def whole(a):
        return pl.BlockSpec(a.shape, lambda i: (0,) * a.ndim)

    smalls = [dw_w.astype(F32), dw_b.reshape(1, PART).astype(F32), ln_g.reshape(1, PART).astype(F32),
              ln_b.reshape(1, PART).astype(F32), sconv_w.astype(F32)]
    specs = [prev(0), cur(0), nxt(0), prev(1), cur(1), nxt(1), cur(2),
             prev(3), cur(3), nxt(3), prev(4), cur(4), nxt(4)] + [whole(a) for a in smalls]
    return pl.pallas_call(
        functools.partial(_conv_mix_kernel, tm=tm, tiles_per_seq=seq // tm),
        out_shape=jax.ShapeDtypeStruct((t, 2 * PART), BF16),
        grid=(t // tm,),
        in_specs=specs,
        out_specs=pl.BlockSpec((tm, 2 * PART), lambda i: (i, 0)),
        scratch_shapes=[pltpu.VMEM((tm + 2 * h, PART), F32), pltpu.VMEM((tm + 2 * h, PART), F32)],
        compiler_params=_cparams("parallel"),
        name="conv_mix",
    )(*([parts] * 13), *smalls)


FFN_HALO = 8


def _ffn_kernel(xp_ref, xc_ref, xn_ref, wg_ref, wu_ref, dg_ref, du_ref, wd_ref, g_ref, b_ref,
                out_ref, xe_ref, acc_ref, *, tm, tiles_per_seq, alpha):
    h = FFN_HALO
    i = pl.program_id(0)
    j = pl.program_id(1)

    @pl.when(j == 0)
    def _():
        keep_prev = jnp.where(i % tiles_per_seq == 0, 0.0, 1.0).astype(F32)
        keep_next = jnp.where(i % tiles_per_seq == tiles_per_seq - 1, 0.0, 1.0).astype(F32)
        xe_ref[0:h, :] = (xp_ref[...] * keep_prev).astype(BF16)
        xe_ref[h:h + tm, :] = xc_ref[...].astype(BF16)
        xe_ref[h + tm:, :] = (xn_ref[...] * keep_next).astype(BF16)
        acc_ref[...] = jnp.zeros_like(acc_ref)

    xe = xe_ref[...]
    rows = tm + 2 * h

    def conv(hid, dw_ref):
        below = pltpu.roll(hid, 1, 0)
        above = pltpu.roll(hid, rows - 1, 0)
        c = dw_ref[0:1, :] * below + dw_ref[1:2, :] * hid + dw_ref[2:3, :] * above
        return c[h:h + tm, :]

    cg = conv(jnp.dot(xe, wg_ref[...], preferred_element_type=F32), dg_ref)
    cu = conv(jnp.dot(xe, wu_ref[...], preferred_element_type=F32), du_ref)
    a = (cg * _sigmoid(cg) * cu).astype(BF16)
    acc_ref[...] += jnp.dot(a, wd_ref[...], preferred_element_type=F32)

    @pl.when(j == pl.num_programs(1) - 1)
    def _():
        out_ref[...] = _layer_norm(alpha * xc_ref[...] + acc_ref[...], g_ref[...], b_ref[...])


def _ffn(x, w_up, dw, w_down, g, b, seq, *, alpha, tm, fc):
    t, d = x.shape
    d_ff = w_down.shape[0]
    nch = d_ff // fc
    h = FFN_HALO
    per = tm // h
    nh = t // h
    vec = pl.BlockSpec((1, d), lambda i, j: (0, 0))
    dw = dw.astype(F32)
    return pl.pallas_call(
        functools.partial(_ffn_kernel, tm=tm, tiles_per_seq=seq // tm, alpha=alpha),
        out_shape=jax.ShapeDtypeStruct((t, d), F32),
        grid=(t // tm, nch),
        in_specs=[pl.BlockSpec((h, d), lambda i, j: (jnp.maximum(i * per - 1, 0), 0)),
                  pl.BlockSpec((tm, d), lambda i, j: (i, 0)),
                  pl.BlockSpec((h, d), lambda i, j: (jnp.minimum((i + 1) * per, nh - 1), 0)),
                  pl.BlockSpec((d, fc), lambda i, j: (0, j)),
                  pl.BlockSpec((d, fc), lambda i, j: (0, nch + j)),
                  pl.BlockSpec((FFN_K, fc), lambda i, j: (0, j)),
                  pl.BlockSpec((FFN_K, fc), lambda i, j: (0, nch + j)),
                  pl.BlockSpec((fc, d), lambda i, j: (j, 0)),
                  vec, vec],
        out_specs=pl.BlockSpec((tm, d), lambda i, j: (i, 0)),
        scratch_shapes=[pltpu.VMEM((tm + 2 * h, d), BF16), pltpu.VMEM((tm, d), F32)],
        compiler_params=_cparams("parallel", "arbitrary"),
        name="conv_ffn",
    )(x, x, x, w_up, w_up, dw, dw, w_down, g.reshape(1, d), b.reshape(1, d))


def kernel(x, t5_bias, attn_w_in, attn_w_out, na_rpb, conv_w_in, conf_dw_w, conf_dw_b, conf_ln_g,
           conf_ln_b, sconv_w, conv_w_out, ffn_w_up, ffn_dw_w, ffn_w_down, mix_ln_g, mix_ln_b,
           ffn_ln_g, ffn_ln_b):
    batch, seq, d_model = x.shape
    depth = ffn_w_up.shape[0]
    alpha = (2 * depth) ** 0.25
    q_scale = HEAD_DIM ** -0.5
    xt = x.reshape(batch * seq, d_model)
    dil_tabs = [_dil_bias_table(t5_bias, w, d) for (w, d) in DIL_PATTERNS]

    for i in range(depth):
        j = i // 2
        if i % 2 == 0:
            parts = _proj_parts(xt, attn_w_in[j].astype(BF16), tm=1024,
                                scaled_parts=(0, 3), scale=q_scale)
            ya = _na_attention(parts, _na_bias_table(na_rpb[j]), batch, seq)
            dil_outs = [_dil_attention(parts, tab, batch, seq, d, rb=512)
                        for tab, (_, d) in zip(dil_tabs, DIL_PATTERNS)]
            xt = _attn_out(ya, dil_outs, xt, attn_w_out[j].astype(BF16), mix_ln_g[i], mix_ln_b[i],
                           alpha=alpha, tm=512)
        else:
            parts = _proj_parts(xt, conv_w_in[j].astype(BF16), tm=1024)
            mixed = _conv_mix(parts, conf_dw_w[j], conf_dw_b[j], conf_ln_g[j], conf_ln_b[j],
                              sconv_w[j], seq, tm=512)
            xt = _conv_out(mixed, xt, conv_w_out[j].astype(BF16), mix_ln_g[i], mix_ln_b[i],
                           alpha=alpha, tm=512)
        xt = _ffn(xt, ffn_w_up[i].astype(BF16), ffn_dw_w[i], ffn_w_down[i].astype(BF16),
                  ffn_ln_g[i], ffn_ln_b[i], seq, alpha=alpha, tm=1024, fc=256)
    return xt.reshape(batch, seq, d_model)
```

```python
import functools
import math

import jax
import jax.numpy as jnp
from jax import lax
from jax.experimental import pallas as pl
from jax.experimental.pallas import tpu as pltpu

GRID_W = 64
HEAD_DIM = 64
N_HEADS = 8
NA_KH = 8
NA_KW = 16
DIL_PATTERNS = ((128, 1), (512, 4), (2048, 16))
DIL_BLOCK = 64
N_BUCKETS = 32
T5_MAX_DIST = 1024
CONF_K = 31
SC_K = 3
FFN_K = 3
LN_EPS = 1e-5
NEG = -1e30
PART = N_HEADS * HEAD_DIM

LANES = 128
SUBLANES_F32 = 8
SUBLANES_BF16 = 16
VMEM_LIMIT = 48 * 1024 * 1024
BIG_VMEM_LIMIT = 56 * 1024 * 1024

N_PAIRS = PART // LANES

F32 = jnp.float32
BF16 = jnp.bfloat16


def _cparams(*sem, vmem=VMEM_LIMIT):
    return pltpu.CompilerParams(dimension_semantics=sem, vmem_limit_bytes=vmem)


def _layer_norm(y, g, b):
    mu = jnp.mean(y, axis=-1, keepdims=True)
    d = y - mu
    var = jnp.mean(d * d, axis=-1, keepdims=True)
    return d * lax.rsqrt(var + LN_EPS) * g + b


def _sigmoid(x):
    return 1.0 / (1.0 + jnp.exp(-x))


def _proj_kernel(x_ref, w_ref, o_ref, *rest, scaled_parts, scale, dil_from, dils):
    res_refs, (xb_ref, slab_ref) = rest[:len(dils)], rest[len(dils):]
    j = pl.program_id(1)
    tm = x_ref.shape[0]

    @pl.when(j == 0)
    def _():
        xb_ref[...] = x_ref[...].astype(BF16)

    y = jnp.dot(xb_ref[...], w_ref[...], preferred_element_type=F32)
    if scaled_parts:
        is_scaled = functools.reduce(jnp.logical_or, [j == p for p in scaled_parts])
        y = y * jnp.where(is_scaled, scale, 1.0).astype(F32)
    o_ref[...] = y.astype(o_ref.dtype)

    if dils:
        @pl.when(j >= dil_from)
        def _():
            for c in range(N_PAIRS):
                slab_ref[c] = y[:, c * LANES:(c + 1) * LANES]
            for d, res_ref in zip(dils, res_refs):
                for r in range(d):
                    for c in range(N_PAIRS):
                        res_ref[r, :, c * LANES:(c + 1) * LANES] = (
                            slab_ref[c, pl.ds(r, tm // d, stride=d), :].astype(res_ref.dtype))


def _proj_parts(x, w, batch, *, tm, scaled_parts=(), scale=1.0, dil_from=0, dils=()):
    t, k = x.shape
    seq = t // batch
    n_parts = w.shape[1] // PART
    tiles_per_seq = seq // tm
    out_shape = [jax.ShapeDtypeStruct((n_parts, t, PART), BF16)]
    out_specs = [pl.BlockSpec((None, tm, PART), lambda i, j: (j, i, 0))]
    for d in dils:
        out_shape.append(jax.ShapeDtypeStruct((n_parts - dil_from, batch, d, seq // d, PART), BF16))
        out_specs.append(pl.BlockSpec(
            (None, None, d, tm // d, PART),
            lambda i, j: (jnp.maximum(j - dil_from, 0), i // tiles_per_seq, 0, i % tiles_per_seq, 0)))
    outs = pl.pallas_call(
        functools.partial(_proj_kernel, scaled_parts=scaled_parts, scale=scale, dil_from=dil_from,
                          dils=tuple(dils)),
        out_shape=out_shape,
        grid=(t // tm, n_parts),
        in_specs=[pl.BlockSpec((tm, k), lambda i, j: (i, 0)),
                  pl.BlockSpec((k, PART), lambda i, j: (0, j))],
        out_specs=out_specs,
        scratch_shapes=[pltpu.VMEM((tm, k), BF16), pltpu.VMEM((N_PAIRS, tm, LANES), F32)],
        compiler_params=_cparams("parallel", "arbitrary"),
        name="proj_parts",
    )(x, w)
    return outs


def _stack_pair(qp, lo):
    zero = jnp.zeros_like(qp)
    return jnp.concatenate([jnp.where(lo, qp, zero), jnp.where(lo, zero, qp)], axis=0)


def _pair_scores(qp, kp, bias, lo):
    s = lax.dot_general(_stack_pair(qp, lo), kp, (((1,), (1,)), ((), ())), preferred_element_type=F32)
    return s + bias


def _pair_softmax_pv(s, vp, lo):
    n = s.shape[0] // 2
    m = jnp.max(s, axis=-1, keepdims=True)
    p = jnp.exp(s - m)
    l = jnp.sum(p, axis=-1, keepdims=True)
    o = jnp.dot(p.astype(BF16), vp, preferred_element_type=F32) / l
    lse = jnp.broadcast_to(m + jnp.log(l), (2 * n, LANES))
    return jnp.where(lo, o[:n], o[n:]), jnp.where(lo, lse[:n], lse[n:])


NA_ROWS_PER_STEP = 8


def _na_kernel(q_ref, k_ref, v_ref, b_ref, o_ref, *, rows):
    rps = q_ref.shape[0]
    band = rps + NA_KH - 1
    g = pl.program_id(1)
    band0 = jnp.clip(g * rps - NA_KH // 2, 0, rows - band)
    lo = lax.broadcasted_iota(jnp.int32, (GRID_W, LANES), 1) < HEAD_DIM
    n_keys = NA_KH * GRID_W
    lanes = [slice(j * LANES, (j + 1) * LANES) for j in range(N_PAIRS)]

    def key_row(rr):
        r = g * rps + rr
        r0 = jnp.clip(r - NA_KH // 2, 0, rows - NA_KH)
        return r0 - band0, r - r0

    def scores(rr):
        off, var = key_row(rr)
        return [_pair_scores(q_ref[rr, :, sl], k_ref[0, 0, pl.ds(off, NA_KH), :, sl].reshape(n_keys, LANES),
                             b_ref[var, j], lo) for j, sl in enumerate(lanes)]

    def finish(rr, s_list):
        off, _ = key_row(rr)
        outs = [_pair_softmax_pv(s, v_ref[0, 0, pl.ds(off, NA_KH), :, sl].reshape(n_keys, LANES), lo)[0]
                for s, sl in zip(s_list, lanes)]
        o_ref[rr] = jnp.concatenate(outs, axis=1).astype(o_ref.dtype)

    s_next = scores(0)
    for rr in range(rps):
        s_cur = s_next
        if rr + 1 < rps:
            s_next = scores(rr + 1)
        finish(rr, s_cur)


def _one_hot_rows(idx, n):
    return (idx[..., None] == jnp.arange(n)).astype(F32)


def _na_bias_table(rpb):
    c = jnp.arange(GRID_W)
    c0 = jnp.clip(c - NA_KW // 2, 0, GRID_W - NA_KW)
    valid = (c[None, :] >= c0[:, None]) & (c[None, :] < c0[:, None] + NA_KW)
    col_rel = c[None, :] - c[:, None] + NA_KW - 1
    sel = _one_hot_rows(col_rel, 2 * NA_KW - 1)
    t = jnp.einsum("hrk,cjk->hrcj", rpb.astype(F32), sel, precision=lax.Precision.HIGHEST)
    t = jnp.where(valid[None, None], t, NEG)
    tab = jnp.stack([t[:, NA_KH - 1 - v:2 * NA_KH - 1 - v] for v in range(NA_KH)])
    tab = jnp.transpose(tab, (0, 1, 3, 2, 4))
    return tab.reshape(NA_KH, N_PAIRS, 2 * GRID_W, NA_KH * GRID_W)


def _na_attention(parts, bias_tab, batch, seq):
    rows = seq // GRID_W
    rps = NA_ROWS_PER_STEP
    band = rps + NA_KH - 1
    pv = parts.reshape(parts.shape[0], batch, rows, GRID_W, PART)

    def band0(g):
        return jnp.clip(g * rps - NA_KH // 2, 0, rows - band)

    kv_block = (pl.Element(1), pl.Element(1), pl.Element(band), pl.Element(GRID_W), pl.Element(PART))
    out = pl.pallas_call(
        functools.partial(_na_kernel, rows=rows),
        out_shape=jax.ShapeDtypeStruct((batch, rows, GRID_W, PART), BF16),
        grid=(batch, rows // rps),
        in_specs=[pl.BlockSpec((None, None, rps, GRID_W, PART), lambda b, g: (0, b, g, 0, 0)),
                  pl.BlockSpec(kv_block, lambda b, g: (1, b, band0(g), 0, 0)),
                  pl.BlockSpec(kv_block, lambda b, g: (2, b, band0(g), 0, 0)),
                  pl.BlockSpec(bias_tab.shape, lambda b, g: (0, 0, 0, 0))],
        out_specs=pl.BlockSpec((None, rps, GRID_W, PART), lambda b, g: (b, g, 0, 0)),
        compiler_params=_cparams("parallel", "arbitrary"),
        name="na_attention",
    )(pv, pv, pv, bias_tab)
    return out.reshape(batch * seq, PART)


DIL_TILE = DIL_BLOCK * max(d for _, d in DIL_PATTERNS)


def _dil_kernel(*refs, n_tiles):
    dils = [d for _, d in DIL_PATTERNS]
    np_ = len(dils)
    qkv = [refs[7 * p:7 * p + 7] for p in range(np_)]
    bias_refs = refs[7 * np_:8 * np_]
    o_ref = refs[8 * np_]
    slabs = refs[8 * np_ + 1:]
    slab_of = {d: (slabs[2 * i], slabs[2 * i + 1]) for i, d in enumerate([d for d in dils if d > 1])}

    blk = DIL_BLOCK
    n = pl.program_id(1)
    lo = lax.broadcasted_iota(jnp.int32, (blk, LANES), 1) < HEAD_DIM
    col = lax.broadcasted_iota(jnp.int32, (2 * blk, 3 * blk), 1)
    lanes = [slice(j * LANES, (j + 1) * LANES) for j in range(N_PAIRS)]
    edge_lo = jnp.where(col < blk, jnp.where(n == 0, NEG, 0.0).astype(F32), 0.0)
    edge_hi = jnp.where(col >= 2 * blk, jnp.where(n == n_tiles - 1, NEG, 0.0).astype(F32), 0.0)

    def rows_of(ref_p, ref_c, ref_n, r, t, subs, sl):
        if subs == 1:
            return jnp.concatenate([ref_p[r, :, sl], ref_c[r, :, sl], ref_n[r, :, sl]], axis=0)
        if t == 0:
            return jnp.concatenate([ref_p[r, :, sl], ref_c[r, 0:2 * blk, sl]], axis=0)
        if t == subs - 1:
            return jnp.concatenate([ref_c[r, (t - 1) * blk:(t + 1) * blk, sl], ref_n[r, :, sl]], axis=0)
        return ref_c[r, (t - 1) * blk:(t + 2) * blk, sl]

    biases = {}

    def bias_of(p, t, subs, j):
        key = (p, t == 0, t == subs - 1, j)
        if key not in biases:
            b = bias_refs[p][j]
            if t == 0:
                b = b + edge_lo
            if t == subs - 1:
                b = b + edge_hi
            biases[key] = b
        return biases[key]

    def scores(unit):
        p, r, t = unit
        q_ref, kp_ref, kc_ref, kn_ref = qkv[p][:4]
        subs = DIL_TILE // dils[p] // blk
        return [_pair_scores(q_ref[r, t * blk:(t + 1) * blk, sl], rows_of(kp_ref, kc_ref, kn_ref, r, t, subs, sl),
                             bias_of(p, t, subs, j), lo) for j, sl in enumerate(lanes)]

    def finish(unit, s_list):
        p, r, t = unit
        d = dils[p]
        vp_ref, vc_ref, vn_ref = qkv[p][4:]
        subs = DIL_TILE // d // blk
        outs = []
        for j, (s, sl) in enumerate(zip(s_list, lanes)):
            o, lse = _pair_softmax_pv(s, rows_of(vp_ref, vc_ref, vn_ref, r, t, subs, sl), lo)
            if d > 1:
                o_s, l_s = slab_of[d]
                o_s[j, pl.ds(r + d * blk * t, blk, stride=d), :] = o
                l_s[j, pl.ds(r + d * blk * t, blk, stride=d), :] = lse
            else:
                rows = slice(t * blk, (t + 1) * blk)
                os_, ls_ = [o], [lse]
                for dd in dils:
                    if dd > 1:
                        os_.append(slab_of[dd][0][j, rows, :])
                        ls_.append(slab_of[dd][1][j, rows, :])
                m = functools.reduce(jnp.maximum, ls_)
                es = [jnp.exp(l - m) for l in ls_]
                num = functools.reduce(jnp.add, [e * oo for e, oo in zip(es, os_)])
                outs.append(num / functools.reduce(jnp.add, es))
        if d == 1:
            o_ref[t * blk:(t + 1) * blk, :] = jnp.concatenate(outs, axis=1).astype(o_ref.dtype)

    units = []
    for p in sorted(range(np_), key=lambda p: -dils[p]):
        subs = DIL_TILE // dils[p] // blk
        units += [(p, r, t) for r in range(dils[p]) for t in range(subs)]
    s_next = scores(units[0])
    for u, unit in enumerate(units):
        s_cur = s_next
        if u + 1 < len(units):
            s_next = scores(units[u + 1])
        finish(unit, s_cur)


def _t5_bucket(rel):
    nb = N_BUCKETS // 2
    max_exact = nb // 2
    ret = jnp.where(rel > 0, nb, 0)
    n = jnp.abs(rel)
    large = max_exact + (jnp.log(jnp.maximum(n, 1).astype(F32) / max_exact)
                         / math.log(T5_MAX_DIST / max_exact) * (nb - max_exact)).astype(jnp.int32)
    large = jnp.minimum(large, nb - 1)
    return ret + jnp.where(n < max_exact, n, large)


def _dil_bias_table(t5_table, window, dil):
    half = window // (2 * dil)
    qi = jnp.arange(DIL_BLOCK)
    ki = jnp.arange(3 * DIL_BLOCK)
    rel = ki[None, :] - DIL_BLOCK - qi[:, None]
    sel = _one_hot_rows(_t5_bucket(rel * dil), N_BUCKETS)
    bias = jnp.einsum("qkb,bh->hqk", sel, t5_table.astype(F32), precision=lax.Precision.HIGHEST)
    bias = jnp.where((jnp.abs(rel) <= half)[None], bias, NEG)
    return bias.reshape(N_PAIRS, 2 * DIL_BLOCK, 3 * DIL_BLOCK)


def _dil_attention(layouts, bias_tabs, batch, seq):
    n_tiles = seq // DIL_TILE
    operands, in_specs = [], []
    for (arr, q_part), (_, d) in zip(layouts, DIL_PATTERNS):
        rows = DIL_TILE // d
        subs = rows // DIL_BLOCK
        n64 = seq // d // DIL_BLOCK
        cur = (None, None, d, rows, PART)
        halo = (None, None, d, DIL_BLOCK, PART)

        def cur_spec(part):
            return pl.BlockSpec(cur, lambda b, n, part=part: (part, b, 0, n, 0))

        def prev_spec(part, subs=subs):
            return pl.BlockSpec(halo, lambda b, n, part=part: (part, b, 0, jnp.maximum(n * subs - 1, 0), 0))

        def next_spec(part, subs=subs, n64=n64):
            return pl.BlockSpec(halo, lambda b, n, part=part: (part, b, 0, jnp.minimum((n + 1) * subs, n64 - 1), 0))

        in_specs += [cur_spec(q_part),
                     prev_spec(q_part + 1), cur_spec(q_part + 1), next_spec(q_part + 1),
                     prev_spec(q_part + 2), cur_spec(q_part + 2), next_spec(q_part + 2)]
        operands += [arr] * 7
    in_specs += [pl.BlockSpec(tab.shape, lambda b, n: (0, 0, 0)) for tab in bias_tabs]
    operands += list(bias_tabs)
    n_slabs = 2 * sum(d > 1 for _, d in DIL_PATTERNS)
    return pl.pallas_call(
        functools.partial(_dil_kernel, n_tiles=n_tiles),
        out_shape=jax.ShapeDtypeStruct((batch * seq, PART), BF16),
        grid=(batch, n_tiles),
        in_specs=in_specs,
        out_specs=pl.BlockSpec((DIL_TILE, PART), lambda b, n: (b * n_tiles + n, 0)),
        scratch_shapes=[pltpu.VMEM((N_PAIRS, DIL_TILE, LANES), F32)] * n_slabs,
        compiler_params=_cparams("parallel", "arbitrary", vmem=BIG_VMEM_LIMIT),
        name="dil_attention",
    )(*operands)


def _mix_out_kernel(*refs, alpha, widths):
    acts, (x_ref, w_ref, g_ref, b_ref, out_ref) = refs[:len(widths)], refs[len(widths):]
    y, c0 = None, 0
    for a_ref, wdt in zip(acts, widths):
        part = jnp.dot(a_ref[...], w_ref[c0:c0 + wdt, :], preferred_element_type=F32)
        y = part if y is None else y + part
        c0 += wdt
    out_ref[...] = _layer_norm(alpha * x_ref[...] + y, g_ref[...], b_ref[...])


def _mix_out(acts, x, w, g, b, *, alpha, tm):
    t, d = x.shape
    widths = tuple(a.shape[1] for a in acts)
    full = pl.BlockSpec((tm, d), lambda i: (i, 0))
    vec = pl.BlockSpec((1, d), lambda i: (0, 0))
    return pl.pallas_call(
        functools.partial(_mix_out_kernel, alpha=alpha, widths=widths),
        out_shape=jax.ShapeDtypeStruct((t, d), F32),
        grid=(t // tm,),
        in_specs=[pl.BlockSpec((tm, wdt), lambda i: (i, 0)) for wdt in widths]
        + [full, pl.BlockSpec(w.shape, lambda i: (0, 0)), vec, vec],
        out_specs=full,
        compiler_params=_cparams("parallel"),
        name="mix_out_ln",
    )(*acts, x, w, g.reshape(1, d), b.reshape(1, d))


CONV_HALO = 16


def _dwconv_rows(buf_ref, w_ref, n_taps, tm):
    rows = buf_ref.shape[0]
    first = CONV_HALO - n_taps // 2
    acc = None
    for res in range(SUBLANES_F32):
        taps = [k for k in range(n_taps) if (first + k) % SUBLANES_F32 == res]
        if not taps:
            continue
        buf = buf_ref[...]
        shifted = buf if res == 0 else pltpu.roll(buf, rows - res, 0)
        for k in taps:
            base = first + k - res
            term = w_ref[k:k + 1, :] * shifted[base:base + tm, :]
            acc = term if acc is None else acc + term
    return acc


def _conv_mix_kernel(ca_p, ca_c, ca_n, cg_p, cg_c, cg_n, gb_c, gc_p, gc_c, gc_n, hx_p, hx_c, hx_n,
                     dw_ref, dwb_ref, lg_ref, lb_ref, sw_ref, o_ref, ub_ref, zb_ref, *, tm, tiles_per_seq):
    h = CONV_HALO
    i = pl.program_id(0)
    keep_prev = jnp.where(i % tiles_per_seq == 0, 0.0, 1.0).astype(F32)
    keep_next = jnp.where(i % tiles_per_seq == tiles_per_seq - 1, 0.0, 1.0).astype(F32)

    def glu(a, g):
        return a[...].astype(F32) * _sigmoid(g[...].astype(F32))

    ub_ref[0:h, :] = glu(ca_p, cg_p) * keep_prev
    ub_ref[h:h + tm, :] = glu(ca_c, cg_c)
    ub_ref[h + tm:, :] = glu(ca_n, cg_n) * keep_next

    def gate(a, g):
        return a[...].astype(F32) * g[...].astype(F32)

    zb_ref[0:h, :] = gate(gc_p, hx_p) * keep_prev
    zb_ref[h:h + tm, :] = gate(gc_c, hx_c)
    zb_ref[h + tm:, :] = gate(gc_n, hx_n) * keep_next

    u = _layer_norm(_dwconv_rows(ub_ref, dw_ref, CONF_K, tm) + dwb_ref[...], lg_ref[...], lb_ref[...])
    u = u * _sigmoid(u)
    z = gb_c[...].astype(F32) * _dwconv_rows(zb_ref, sw_ref, SC_K, tm)

    o_ref[:, 0:PART] = u.astype(o_ref.dtype)
    o_ref[:, PART:] = z.astype(o_ref.dtype)


def _conv_mix(parts, dw_w, dw_b, ln_g, ln_b, sconv_w, seq, *, tm):
    t = parts.shape[1]
    h = CONV_HALO
    per = tm // h
    nh = t // h

    def cur(part):
        return pl.BlockSpec((None, tm, PART), lambda i: (part, i, 0))

    def prev(part):
        return pl.BlockSpec((None, h, PART), lambda i: (part, jnp.maximum(i * per - 1, 0), 0))

    def nxt(part):
        return pl.BlockSpec((None, h, PART), lambda i: (part, jnp.minimum((i + 1) * per, nh - 1), 0))

    def whole(a):
        return pl.BlockSpec(a.shape, lambda i: (0,) * a.ndim)

    smalls = [dw_w.astype(F32), dw_b.reshape(1, PART).astype(F32), ln_g.reshape(1, PART).astype(F32),
              ln_b.reshape(1, PART).astype(F32), sconv_w.astype(F32)]
    specs = [prev(0), cur(0), nxt(0), prev(1), cur(1), nxt(1), cur(2),
             prev(3), cur(3), nxt(3), prev(4), cur(4), nxt(4)] + [whole(a) for a in smalls]
    return pl.pallas_call(
        functools.partial(_conv_mix_kernel, tm=tm, tiles_per_seq=seq // tm),
        out_shape=jax.ShapeDtypeStruct((t, 2 * PART), BF16),
        grid=(t // tm,),
        in_specs=specs,
        out_specs=pl.BlockSpec((tm, 2 * PART), lambda i: (i, 0)),
        scratch_shapes=[pltpu.VMEM((tm + 2 * h, PART), F32), pltpu.VMEM((tm + 2 * h, PART), F32)],
        compiler_params=_cparams("parallel"),
        name="conv_mix",
    )(*([parts] * 13), *smalls)


FFN_HALO = 8
FFN_ROW_PIECE = 256


def _ffn_kernel(xp_ref, xc_ref, xn_ref, wu_ref, dw_ref, wd_ref, g_ref, b_ref, out_ref,
                xe_ref, ha_ref, hb_ref, acc_ref, *, tm, tiles_per_seq, alpha):
    h = FFN_HALO
    nch = wu_ref.shape[0]
    fc = wd_ref.shape[1]
    rows = tm + 2 * h
    i = pl.program_id(0)
    keep_prev = jnp.where(i % tiles_per_seq == 0, 0.0, 1.0).astype(F32)
    keep_next = jnp.where(i % tiles_per_seq == tiles_per_seq - 1, 0.0, 1.0).astype(F32)
    xe_ref[0:h, :] = (xp_ref[...] * keep_prev).astype(BF16)
    xe_ref[h:h + tm, :] = xc_ref[...].astype(BF16)
    xe_ref[h + tm:, :] = (xn_ref[...] * keep_next).astype(BF16)

    mc = FFN_ROW_PIECE
    n_pieces = tm // mc

    def up_piece(j, m, dst_ref):
        r0 = m * mc
        r1 = rows if m == n_pieces - 1 else r0 + mc
        dst_ref[r0:r1, :] = jnp.dot(xe_ref[r0:r1, :], wu_ref[j], preferred_element_type=F32)

    def gated_piece(src_ref, j, m):
        hid = src_ref[m * mc:(m + 1) * mc + 2 * h, :]
        w = dw_ref[j]
        below = pltpu.roll(hid, 1, 0)
        above = pltpu.roll(hid, mc + 2 * h - 1, 0)
        c = (w[0:1, :] * below + w[1:2, :] * hid + w[2:3, :] * above)[h:h + mc, :]
        cg, cu = c[:, :fc], c[:, fc:]
        return (cg * _sigmoid(cg) * cu).astype(BF16)

    def down_piece(a, j, m):
        acc_ref[m * mc:(m + 1) * mc, :] += jnp.dot(a, wd_ref[j], preferred_element_type=F32)

    def chunk(j, src_ref, dst_ref):
        for m in range(n_pieces):
            if dst_ref is not None:
                up_piece(j + 1, m, dst_ref)
            down_piece(gated_piece(src_ref, j, m), j, m)

    for m in range(n_pieces):
        up_piece(0, m, ha_ref)
    acc_ref[...] = jnp.zeros_like(acc_ref)

    def two_chunks(k, carry):
        chunk(2 * k, ha_ref, hb_ref)
        chunk(2 * k + 1, hb_ref, ha_ref)
        return carry

    lax.fori_loop(0, (nch - 1) // 2, two_chunks, 0)
    chunk(nch - 1, ha_ref, None)
    out_ref[...] = _layer_norm(alpha * xc_ref[...] + acc_ref[...], g_ref[...], b_ref[...])


def _ffn(x, w_up, dw, w_down, g, b, seq, *, alpha, tm, fc):
    t, d = x.shape
    d_ff = w_down.shape[0]
    nch = d_ff // fc
    assert nch % 2 == 1, "the chunk pipeline handles two chunks per trip plus a final one"
    h = FFN_HALO
    per = tm // h
    nh = t // h
    wu = jnp.transpose(w_up.reshape(d, 2, nch, fc), (2, 0, 1, 3)).reshape(nch, d, 2 * fc).astype(BF16)
    dwc = jnp.transpose(dw.astype(F32).reshape(FFN_K, 2, nch, fc), (2, 0, 1, 3)).reshape(nch, FFN_K, 2 * fc)
    wd = w_down.reshape(nch, fc, d).astype(BF16)
    vec = pl.BlockSpec((1, d), lambda i: (0, 0))

    def resident(a):
        return pl.BlockSpec(a.shape, lambda i: (0,) * a.ndim, pipeline_mode=pl.Buffered(1))

    return pl.pallas_call(
        functools.partial(_ffn_kernel, tm=tm, tiles_per_seq=seq // tm, alpha=alpha),
        out_shape=jax.ShapeDtypeStruct((t, d), F32),
        grid=(t // tm,),
        in_specs=[pl.BlockSpec((h, d), lambda i: (jnp.maximum(i * per - 1, 0), 0)),
                  pl.BlockSpec((tm, d), lambda i: (i, 0)),
                  pl.BlockSpec((h, d), lambda i: (jnp.minimum((i + 1) * per, nh - 1), 0)),
                  resident(wu), resident(dwc), resident(wd), vec, vec],
        out_specs=pl.BlockSpec((tm, d), lambda i: (i, 0)),
        scratch_shapes=[pltpu.VMEM((tm + 2 * h, d), BF16),
                        pltpu.VMEM((tm + 2 * h, 2 * fc), F32), pltpu.VMEM((tm + 2 * h, 2 * fc), F32),
                        pltpu.VMEM((tm, d), F32)],
        compiler_params=_cparams("parallel", vmem=BIG_VMEM_LIMIT),
        name="conv_ffn",
    )(x, x, x, wu, dwc, wd, g.reshape(1, d), b.reshape(1, d))


def kernel(x, t5_bias, attn_w_in, attn_w_out, na_rpb, conv_w_in, conf_dw_w, conf_dw_b, conf_ln_g,
           conf_ln_b, sconv_w, conv_w_out, ffn_w_up, ffn_dw_w, ffn_w_down, mix_ln_g, mix_ln_b,
           ffn_ln_g, ffn_ln_b):
    batch, seq, d_model = x.shape
    depth = ffn_w_up.shape[0]
    alpha = (2 * depth) ** 0.25
    q_scale = HEAD_DIM ** -0.5
    xt = x.reshape(batch * seq, d_model)
    dil_tabs = [_dil_bias_table(t5_bias, w, d) for (w, d) in DIL_PATTERNS]
    res_dils = tuple(d for _, d in DIL_PATTERNS if d > 1)

    for i in range(depth):
        j = i // 2
        if i % 2 == 0:
            parts, *res_major = _proj_parts(xt, attn_w_in[j].astype(BF16), batch, tm=DIL_TILE,
                                            scaled_parts=(0, 3), scale=q_scale, dil_from=3, dils=res_dils)
            ya = _na_attention(parts, _na_bias_table(na_rpb[j]), batch, seq)
            natural = parts.reshape(parts.shape[0], batch, 1, seq, PART)
            res_of = dict(zip(res_dils, res_major))
            layouts = [(natural, 3) if d == 1 else (res_of[d], 0) for _, d in DIL_PATTERNS]
            yb = _dil_attention(layouts, dil_tabs, batch, seq)
            xt = _mix_out([ya, yb], xt, attn_w_out[j].astype(BF16), mix_ln_g[i], mix_ln_b[i],
                          alpha=alpha, tm=512)
        else:
            (parts,) = _proj_parts(xt, conv_w_in[j].astype(BF16), batch, tm=1024)
            mixed = _conv_mix(parts, conf_dw_w[j], conf_dw_b[j], conf_ln_g[j], conf_ln_b[j],
                              sconv_w[j], seq, tm=512)
            xt = _mix_out([mixed], xt, conv_w_out[j].astype(BF16), mix_ln_g[i], mix_ln_b[i],
                          alpha=alpha, tm=512)
        xt = _ffn(xt, ffn_w_up[i].astype(BF16), ffn_dw_w[i], ffn_w_down[i].astype(BF16),
                  ffn_ln_g[i], ffn_ln_b[i], seq, alpha=alpha, tm=1024, fc=256)
    return xt.reshape(batch, seq, d_model)
```
